```python
import jax
import jax.numpy as jnp
from jax import lax
import numpy as np

D_MODEL = 1024
BATCH = 1
SEQ = 16384
DEPTH = 1

CHUNK = 64
NORM_EPS = 1e-6
D_FF = 2816
RWKV_HEADS = 16
RWKV_HEAD_DIM = 64
D_RWKV = RWKV_HEADS * RWKV_HEAD_DIM
DECAY_LORA = 64
AAA_LORA = 64
GATE_LORA = 128
GN_EPS = RWKV_HEAD_DIM * 1e-5
ATT_HEADS = 16
ATT_HEAD_DIM = 64
D_ATT = ATT_HEADS * ATT_HEAD_DIM
IDX_HEADS = 8
IDX_HEAD_DIM = 64
INDEX_TOPK = 256
Q_BLOCK = 128
RWKV_COLS = (D_RWKV, D_RWKV, D_RWKV, DECAY_LORA, AAA_LORA, GATE_LORA)
DSA_COLS = (D_ATT, D_ATT, D_ATT, IDX_HEADS * IDX_HEAD_DIM, IDX_HEAD_DIM, IDX_HEADS)
GATE_COLS = (D_MODEL, D_MODEL)
D_SHIFT = sum(RWKV_COLS)
D_IN = sum(RWKV_COLS) + sum(DSA_COLS) + sum(GATE_COLS)

kernel_name = "hybrid_rwkv7_dsa_macaron_block"


def _rms_norm(x, gain):
    x32 = x.astype(jnp.float32)
    y = x32 * lax.rsqrt(jnp.mean(x32 * x32, axis=-1, keepdims=True) + NORM_EPS)
    return (y * gain.astype(jnp.float32)).astype(x.dtype)


def _swiglu(h, w_gate, w_up, w_down):
    return (jax.nn.silu(h @ w_gate) * (h @ w_up)) @ w_down


def _split(t, sizes):
    return jnp.split(t, np.cumsum(sizes)[:-1].tolist(), axis=-1)


def _causal_token_shift(p, mix):
    prev = jnp.pad(p, ((0, 0), (1, 0), (0, 0)))[:, :-1]
    return p + (prev - p) * mix


def _wkv7_scan(r, decay, k, v, kk, a):
    bsz = r.shape[0]

    def step(state, inp):
        r_t, w_t, k_t, v_t, kk_t, a_t = inp
        removal = jnp.einsum('bhij,bhj->bhi', state, kk_t)
        state = (state * w_t[:, :, None, :]
                 - removal[..., None] * (kk_t * a_t)[:, :, None, :]
                 + v_t[..., None] * k_t[:, :, None, :])
        return state, jnp.einsum('bhij,bhj->bhi', state, r_t)

    xs = tuple(jnp.moveaxis(t, 1, 0) for t in (r, decay, k, v, kk, a))
    s0 = jnp.zeros((bsz, RWKV_HEADS, RWKV_HEAD_DIM, RWKV_HEAD_DIM), jnp.float32)
    _, ys = lax.scan(step, s0, xs)
    return jnp.moveaxis(ys, 0, 1)


def _rwkv7_time_mix(r, k, v, xw, xa, xg, decay_lora, decay_base, aaa_lora, aaa_base,
                    gate_lora, k_k, k_a, r_k, gn_weight, gn_bias):
    bsz, seq_len = r.shape[0], r.shape[1]
    f32 = jnp.float32

    def heads(t):
        return t.reshape(bsz, seq_len, RWKV_HEADS, RWKV_HEAD_DIM)

    w = -jax.nn.softplus(-(decay_base + jnp.tanh(xw) @ decay_lora).astype(f32)) - 0.5
    decay = heads(jnp.exp(-jnp.exp(w)))
    a = heads(jax.nn.sigmoid((aaa_base + xa @ aaa_lora).astype(f32)))
    g = jax.nn.sigmoid(xg) @ gate_lora
    r32, k32, v32 = heads(r.astype(f32)), heads(k.astype(f32)), heads(v.astype(f32))
    kk = k32 * k_k.astype(f32).reshape(RWKV_HEADS, RWKV_HEAD_DIM)
    kk = kk / jnp.maximum(jnp.sqrt(jnp.sum(kk * kk, axis=-1, keepdims=True)), 1e-12)
    k32 = k32 * (1.0 + (a - 1.0) * k_a.astype(f32).reshape(RWKV_HEADS, RWKV_HEAD_DIM))
    y = _wkv7_scan(r32, decay, k32, v32, kk, a)
    mu = jnp.mean(y, axis=-1, keepdims=True)
    var = jnp.mean(jnp.square(y - mu), axis=-1, keepdims=True)
    y = ((y - mu) * lax.rsqrt(var + GN_EPS)).reshape(bsz, seq_len, D_RWKV)
    y = y * gn_weight.astype(f32) + gn_bias.astype(f32)
    bonus = jnp.sum(r32 * k32 * r_k.astype(f32), axis=-1, keepdims=True) * v32
    y = (y + bonus.reshape(bsz, seq_len, D_RWKV)) * g.astype(f32)
    return y.astype(r.dtype)


def _dsa_attention(q, k, v, iq, ik, iw):
    bsz, seq_len = q.shape[0], q.shape[1]
    f32 = jnp.float32
    top_k = min(INDEX_TOPK, seq_len // 4)
    n_blocks = seq_len // Q_BLOCK
    q = q.reshape(bsz, seq_len, ATT_HEADS, ATT_HEAD_DIM)
    k = k.reshape(bsz, seq_len, ATT_HEADS, ATT_HEAD_DIM)
    v = v.reshape(bsz, seq_len, ATT_HEADS, ATT_HEAD_DIM)
    iq = iq.reshape(bsz, seq_len, IDX_HEADS, IDX_HEAD_DIM).astype(f32)
    ik = ik.astype(f32)
    iw = iw.astype(f32) * (IDX_HEADS ** -0.5 * IDX_HEAD_DIM ** -0.5)
    key_pos = jnp.arange(seq_len)
    att_scale = ATT_HEAD_DIM ** -0.5
    gather = jax.vmap(lambda table, idx: table[idx])

    def one_block(blk):
        start = blk * Q_BLOCK
        qb = lax.dynamic_slice_in_dim(q, start, Q_BLOCK, axis=1)
        iqb = lax.dynamic_slice_in_dim(iq, start, Q_BLOCK, axis=1)
        iwb = lax.dynamic_slice_in_dim(iw, start, Q_BLOCK, axis=1)
        q_pos = start + jnp.arange(Q_BLOCK)
        visible_end = (q_pos // CHUNK + 1) * CHUNK
        admissible = key_pos[None, :] < visible_end[:, None]
        idx_score = jnp.einsum('bqhs,bqh->bqs',
                               jax.nn.relu(jnp.einsum('bqhd,bsd->bqhs', iqb, ik)), iwb)
        idx_score = jnp.where(admissible[None], idx_score, -jnp.inf)
        sel_score, sel_idx = lax.top_k(idx_score, top_k)
        valid = jnp.isfinite(sel_score)
        k_sel = gather(k, sel_idx)
        v_sel = gather(v, sel_idx)
        logits = jnp.einsum('bqhd,bqkhd->bhqk', qb, k_sel).astype(f32) * att_scale
        logits = jnp.where(valid[:, None], logits, -jnp.inf)
        probs = jax.nn.softmax(logits, axis=-1).astype(v.dtype)
        return jnp.einsum('bhqk,bqkhd->bqhd', probs, v_sel)

    out = lax.map(one_block, jnp.arange(n_blocks))
    return jnp.moveaxis(out, 0, 1).reshape(bsz, seq_len, D_ATT)


def _layer(x, ffn1_norm, ffn1_w_gate, ffn1_w_up, ffn1_w_down, mix_norm, w_in, time_mix,
           decay_lora, decay_base, aaa_lora, aaa_base, gate_lora, k_k, k_a, r_k, gn_weight, gn_bias,
           w_out_rwkv, w_out_attn, w_out, ffn2_norm, ffn2_w_gate, ffn2_w_up, ffn2_w_down):
    x = x + 0.5 * _swiglu(_rms_norm(x, ffn1_norm), ffn1_w_gate, ffn1_w_up, ffn1_w_down)
    h = _rms_norm(x, mix_norm)
    p = h @ w_in
    p_rwkv = _causal_token_shift(p[..., :D_SHIFT], time_mix)
    r, k, v, xw, xa, xg = _split(p_rwkv, RWKV_COLS)
    q_a, k_a_att, v_a, iq, ik, iw, gate_a, gate_b = _split(p[..., D_SHIFT:], DSA_COLS + GATE_COLS)
    y_a = _rwkv7_time_mix(r, k, v, xw, xa, xg, decay_lora, decay_base, aaa_lora, aaa_base,
                          gate_lora, k_k, k_a, r_k, gn_weight, gn_bias) @ w_out_rwkv
    y_b = _dsa_attention(q_a, k_a_att, v_a, iq, ik, iw) @ w_out_attn
    merged = jax.nn.sigmoid(gate_a) * y_a + jax.nn.sigmoid(gate_b) * y_b
    x = x + merged @ w_out
    x = x + 0.5 * _swiglu(_rms_norm(x, ffn2_norm), ffn2_w_gate, ffn2_w_up, ffn2_w_down)
    return x


def setup_inputs(seed: int = 0) -> dict:
    key = jax.random.key(seed)
    keys = jax.random.split(key, 32)
    counter = [0]

    def next_key():
        kk = keys[counter[0]]
        counter[0] += 1
        return kk

    def normal(shape, scale):
        return scale * jax.random.normal(next_key(), shape, jnp.float32)

    def gain(shape):
        return 1.0 + normal(shape, 0.05)

    L = DEPTH
    return {
        'x': normal((BATCH, SEQ, D_MODEL), 1.0),
        'ffn1_norm': gain((L, D_MODEL)),
        'ffn1_w_gate': normal((L, D_MODEL, D_FF), D_MODEL ** -0.5),
        'ffn1_w_up': normal((L, D_MODEL, D_FF), D_MODEL ** -0.5),
        'ffn1_w_down': normal((L, D_FF, D_MODEL), D_FF ** -0.5),
        'mix_norm': gain((L, D_MODEL)),
        'w_in': normal((L, D_MODEL, D_IN), D_MODEL ** -0.5),
        'time_mix': jax.random.uniform(next_key(), (L, D_SHIFT), jnp.float32),
        'decay_lora': normal((L, DECAY_LORA, D_RWKV), 0.5 * DECAY_LORA ** -0.5),
        'decay_base': normal((L, D_RWKV), 1.0) - 1.0,
        'aaa_lora': normal((L, AAA_LORA, D_RWKV), 0.5 * AAA_LORA ** -0.5),
        'aaa_base': normal((L, D_RWKV), 0.5),
        'gate_lora': normal((L, GATE_LORA, D_RWKV), GATE_LORA ** -0.5),
        'k_k': 0.85 + normal((L, D_RWKV), 0.05),
        'k_a': gain((L, D_RWKV)),
        'r_k': normal((L, RWKV_HEADS, RWKV_HEAD_DIM), 0.1),
        'gn_weight': gain((L, D_RWKV)),
        'gn_bias': normal((L, D_RWKV), 0.01),
        'w_out_rwkv': normal((L, D_RWKV, D_MODEL), D_RWKV ** -0.5),
        'w_out_attn': normal((L, D_ATT, D_MODEL), D_ATT ** -0.5),
        'w_out': normal((L, D_MODEL, D_MODEL), D_MODEL ** -0.5),
        'ffn2_norm': gain((L, D_MODEL)),
        'ffn2_w_gate': normal((L, D_MODEL, D_FF), D_MODEL ** -0.5),
        'ffn2_w_up': normal((L, D_MODEL, D_FF), D_MODEL ** -0.5),
        'ffn2_w_down': normal((L, D_FF, D_MODEL), D_FF ** -0.5),
        'final_norm': gain((D_MODEL,)),
    }


def reference(x, ffn1_norm, ffn1_w_gate, ffn1_w_up, ffn1_w_down, mix_norm, w_in, time_mix,
              decay_lora, decay_base, aaa_lora, aaa_base, gate_lora, k_k, k_a, r_k, gn_weight, gn_bias,
              w_out_rwkv, w_out_attn, w_out, ffn2_norm, ffn2_w_gate, ffn2_w_up, ffn2_w_down, final_norm):
    for layer in range(DEPTH):
        x = _layer(x, ffn1_norm[layer], ffn1_w_gate[layer], ffn1_w_up[layer], ffn1_w_down[layer],
                   mix_norm[layer], w_in[layer], time_mix[layer],
                   decay_lora[layer], decay_base[layer], aaa_lora[layer], aaa_base[layer],
                   gate_lora[layer], k_k[layer], k_a[layer], r_k[layer], gn_weight[layer], gn_bias[layer],
                   w_out_rwkv[layer], w_out_attn[layer], w_out[layer],
                   ffn2_norm[layer], ffn2_w_gate[layer], ffn2_w_up[layer], ffn2_w_down[layer])
    return _rms_norm(x, final_norm)
```

```python
import functools

import jax
import jax.numpy as jnp
from jax import lax
from jax.experimental import pallas as pl
from jax.experimental.pallas import tpu as pltpu

F32 = jnp.float32
BF16 = jnp.bfloat16
I32 = jnp.int32

NORM_EPS = 1e-6
CHUNK = 64
RWKV_HEADS = 16
HEAD_DIM = 64
GN_EPS = HEAD_DIM * 1e-5
ATT_HEADS = 16
IDX_HEADS = 8
INDEX_TOPK = 256
LANES = 128
VMEM_LIMIT = 56 * 1024 * 1024

INT_MIN = -2147483648
NEG_BIG = -1e30


def _cparams(*sem):
    return pltpu.CompilerParams(dimension_semantics=sem, vmem_limit_bytes=VMEM_LIMIT)


def _rms(x, gain):
    return x * lax.rsqrt(jnp.mean(x * x, axis=-1, keepdims=True) + NORM_EPS) * gain


def _dot(a, b):
    return jnp.dot(a, b, preferred_element_type=F32)


def _dot_nt(a, b):
    return lax.dot_general(a, b, (((1,), (1,)), ((), ())), preferred_element_type=F32)


def _dot_tn(a, b):
    return lax.dot_general(a, b, (((0,), (0,)), ((), ())), preferred_element_type=F32)


def _ffn_body(x_ref, gin_ref, wg_ref, wu_ref, wd_ref, gout_ref, *rest, final):
    if final:
        o_ref, h_scr, acc_scr = rest
    else:
        xo_ref, ho_ref, h_scr, acc_scr = rest
    j = pl.program_id(1)

    @pl.when(j == 0)
    def _():
        h_scr[...] = _rms(x_ref[...], gin_ref[...]).astype(BF16)
        acc_scr[...] = jnp.zeros_like(acc_scr)

    h = h_scr[...]
    g = _dot(h, wg_ref[...])
    u = _dot(h, wu_ref[...])
    a = g * jax.nn.sigmoid(g) * u
    acc_scr[...] += _dot(a.astype(BF16), wd_ref[...])

    @pl.when(j == pl.num_programs(1) - 1)
    def _():
        xn = x_ref[...] + 0.5 * acc_scr[...]
        if final:
            o_ref[...] = _rms(xn, gout_ref[...])
        else:
            xo_ref[...] = xn
            ho_ref[...] = _rms(xn, gout_ref[...]).astype(BF16)


def _ffn(x, gin, wg, wu, wd, gout, *, final, tm=512, tf=1408):
    m, d = x.shape
    ff = wg.shape[1]
    tm = min(tm, m)
    grid = (m // tm, ff // tf)
    row = pl.BlockSpec((tm, d), lambda i, j: (i, 0))
    vec = pl.BlockSpec((1, d), lambda i, j: (0, 0))
    if final:
        out_shape = jax.ShapeDtypeStruct((m, d), F32)
        out_specs = row
    else:
        out_shape = (jax.ShapeDtypeStruct((m, d), F32), jax.ShapeDtypeStruct((m, d), BF16))
        out_specs = (row, row)
    return pl.pallas_call(
        functools.partial(_ffn_body, final=final),
        grid=grid,
        in_specs=[row, vec,
                  pl.BlockSpec((d, tf), lambda i, j: (0, j)),
                  pl.BlockSpec((d, tf), lambda i, j: (0, j)),
                  pl.BlockSpec((tf, d), lambda i, j: (j, 0)),
                  vec],
        out_specs=out_specs,
        out_shape=out_shape,
        scratch_shapes=[pltpu.VMEM((tm, d), BF16), pltpu.VMEM((tm, d), F32)],
        compiler_params=_cparams("parallel", "arbitrary"),
        name="ffn_final" if final else "ffn_mid",
    )(x, gin.reshape(1, d), wg, wu, wd, gout.reshape(1, d))


def _matmul_body(a_ref, b_ref, o_ref):
    o_ref[...] = _dot(a_ref[...], b_ref[...]).astype(o_ref.dtype)


def _matmul(a, b, out_dtype, *, tm=1024, tn=512, name="proj"):
    m, k = a.shape
    n = b.shape[1]
    tm = min(tm, m)
    tn = min(tn, n)
    return pl.pallas_call(
        _matmul_body,
        grid=(n // tn, m // tm),
        in_specs=[pl.BlockSpec((tm, k), lambda j, i: (i, 0)),
                  pl.BlockSpec((k, tn), lambda j, i: (0, j))],
        out_specs=pl.BlockSpec((tm, tn), lambda j, i: (i, j)),
        out_shape=jax.ShapeDtypeStruct((m, n), out_dtype),
        compiler_params=_cparams("parallel", "parallel"),
        name=name,
    )(a, b)


def _split3(x):
    h1 = x.astype(BF16)
    r1 = x - h1.astype(F32)
    h2 = r1.astype(BF16)
    h3 = (r1 - h2.astype(F32)).astype(BF16)
    return h1, h2, h3


def _head_sum(x, e_ref):
    h1, h2, h3 = _split3(x)
    e = e_ref[...]
    return _dot(h1, e) + _dot(h2, e) + _dot(h3, e)


def _rwkv_body(p_ref, mix_ref, wlora_ref, w0_ref, alora_ref, a0_ref, glora_ref,
               kk_ref, ka_ref, rk_ref, gnw_ref, gnb_ref, e_ref,
               o_ref,
               carry_scr, state_scr, r_scr, lw_scr, k_scr, v_scr, kk_scr, b_scr, y_scr,
               *, tb):
    d = RWKV_HEADS * HEAD_DIM
    t = CHUNK
    i = pl.program_id(0)

    @pl.when(i == 0)
    def _():
        carry_scr[...] = jnp.zeros_like(carry_scr)
        state_scr[...] = jnp.zeros_like(state_scr)

    p = p_ref[...]
    rows = lax.broadcasted_iota(I32, p.shape, 0)
    prev = jnp.where(rows == 0, carry_scr[...], pltpu.roll(p, 1, axis=0))
    carry_scr[...] = p[tb - 1:tb, :]
    ps = p + (prev - p) * mix_ref[...]
    r = ps[:, 0:d]
    k = ps[:, d:2 * d]
    v = ps[:, 2 * d:3 * d]
    xwa = ps[:, 3 * d:3 * d + LANES]
    xg = ps[:, 3 * d + LANES:3 * d + 2 * LANES]
    lane = lax.broadcasted_iota(I32, xwa.shape, 1)
    xwa = jnp.where(lane < 64, jnp.tanh(xwa), xwa).astype(BF16)
    zw = w0_ref[...] + _dot(xwa, wlora_ref[...])
    za = a0_ref[...] + _dot(xwa, alora_ref[...])
    sp = jnp.maximum(-zw, 0.0) + jnp.log(1.0 + jnp.exp(-jnp.abs(zw)))
    lw = -jnp.exp(-sp - 0.5)
    a = jax.nn.sigmoid(za)
    g = _dot(jax.nn.sigmoid(xg).astype(BF16), glora_ref[...])
    kk = k * kk_ref[...]
    nrm = jnp.maximum(jnp.sqrt(_head_sum(kk * kk, e_ref)), 1e-12)
    kk = kk / nrm
    k2 = k * (1.0 + (a - 1.0) * ka_ref[...])
    bonus = _head_sum(r * k2 * rk_ref[...], e_ref) * v
    r_scr[...] = r
    lw_scr[...] = lw
    k_scr[...] = k2
    v_scr[...] = v
    kk_scr[...] = kk
    b_scr[...] = kk * a

    ri = lax.broadcasted_iota(I32, (2 * t, t), 0)
    ci = lax.broadcasted_iota(I32, (2 * t, t), 1)
    tri_mask = ci < jnp.where(ri < t, ri, ri - t + 1)
    ri2 = lax.broadcasted_iota(I32, (t, t), 0)
    ci2 = lax.broadcasted_iota(I32, (t, t), 1)
    tri_incl = (ci2 <= ri2).astype(BF16)
    eye = (ci2 == ri2).astype(F32)

    def chunk_body(c, _):
        rs = pl.ds(pl.multiple_of(c * t, t), t)
        lwc = lw_scr[rs, :]
        l1, l2, l3 = _split3(lwc)
        gc = _dot(tri_incl, l1) + _dot(tri_incl, l2) + _dot(tri_incl, l3)
        gprev = gc - lwc
        gt = gc[t - 1:t, :]
        eneg = jnp.exp(-gc)
        eend = jnp.exp(gt - gc)
        kkc = kk_scr[rs, :]
        bc = b_scr[rs, :]
        kc = k_scr[rs, :]
        vc = v_scr[rs, :].astype(BF16)
        ar = jnp.concatenate([-kkc * jnp.exp(gprev), r_scr[rs, :] * jnp.exp(gc)], axis=0).astype(BF16)
        bt = (bc * eneg).astype(BF16)
        kt = (kc * eneg).astype(BF16)
        bk = jnp.concatenate([bc * eend, kc * eend], axis=0).astype(BF16)
        wt = jnp.exp(gt)
        for h in range(RWKV_HEADS):
            hs = slice(h * HEAD_DIM, (h + 1) * HEAD_DIM)
            ar_h = ar[:, hs]
            v_h = vc[:, hs]
            s_h = state_scr[h]
            ab = jnp.where(tri_mask, _dot_nt(ar_h, bt[:, hs]), 0.0)
            ak = jnp.where(tri_mask, _dot_nt(ar_h, kt[:, hs]), 0.0)
            pw = ab[0:t]
            tinv = eye + pw
            n = 1
            while 2 * n < t:
                pwb = pw.astype(BF16)
                pw = _dot(pwb, pwb)
                tinv = tinv + _dot(tinv.astype(BF16), pw.astype(BF16))
                n *= 2
            xy = _dot_nt(ar_h, s_h.astype(BF16)) + _dot(ak.astype(BF16), v_h)
            u = _dot(tinv.astype(BF16), xy[0:t].astype(BF16))
            ub = u.astype(BF16)
            y = xy[t:2 * t] + _dot(ab[t:2 * t].astype(BF16), ub)
            uv = jnp.concatenate([ub, v_h], axis=0)
            state_scr[h] = s_h * wt[:, hs] + _dot_tn(uv, bk[:, hs])
            mu = jnp.mean(y, axis=-1, keepdims=True)
            yc = y - mu
            var = jnp.mean(yc * yc, axis=-1, keepdims=True)
            y_scr[rs, hs] = yc * lax.rsqrt(var + GN_EPS)
        return 0

    lax.fori_loop(0, tb // t, chunk_body, 0)

    o_ref[...] = ((y_scr[...] * gnw_ref[...] + gnb_ref[...] + bonus) * g).astype(o_ref.dtype)


def _rwkv(p_rwkv, mix, wlora, w0, alora, a0, glora, k_k, k_a, r_k, gnw, gnb, *, tb=256):
    m, dp = p_rwkv.shape
    d = RWKV_HEADS * HEAD_DIM
    tb = min(tb, m)
    head_of = jnp.arange(d) // HEAD_DIM
    e = (head_of[:, None] == head_of[None, :]).astype(BF16)
    vec = lambda n: pl.BlockSpec((1, n), lambda i: (0, 0))
    full = lambda a, b: pl.BlockSpec((a, b), lambda i: (0, 0))
    return pl.pallas_call(
        functools.partial(_rwkv_body, tb=tb),
        grid=(m // tb,),
        in_specs=[pl.BlockSpec((tb, dp), lambda i: (i, 0)), vec(dp),
                  full(LANES, d), vec(d), full(LANES, d), vec(d), full(LANES, d),
                  vec(d), vec(d), vec(d), vec(d), vec(d), full(d, d)],
        out_specs=pl.BlockSpec((tb, d), lambda i: (i, 0)),
        out_shape=jax.ShapeDtypeStruct((m, d), BF16),
        scratch_shapes=[pltpu.VMEM((1, dp), F32),
                        pltpu.VMEM((RWKV_HEADS, HEAD_DIM, HEAD_DIM), F32)]
                       + [pltpu.VMEM((tb, d), F32)] * 7,
        compiler_params=_cparams("arbitrary"),
        name="rwkv7",
    )(p_rwkv, mix.reshape(1, dp), wlora, w0.reshape(1, d), alora, a0.reshape(1, d), glora,
      k_k.reshape(1, d), k_a.reshape(1, d), r_k.reshape(1, d), gnw.reshape(1, d), gnb.reshape(1, d), e)


def _sortable(x):
    b = pltpu.bitcast(x, I32)
    return b ^ ((b >> 31) & 0x7FFFFFFF)


def _idx_topk_body(iq_ref, ikw_ref, ikt_ref, o_ref, key_scr, wb_scr, j_scr, *, tq, tk, top_k):
    seq = o_ref.shape[1]
    qb = pl.program_id(0)
    n_tiles = ((qb + 1) * tq + tk - 1) // tk
    groups = tk // LANES
    row_pos = qb * tq + lax.broadcasted_iota(I32, (tq, LANES), 0)
    vis_end = (row_pos // CHUNK + 1) * CHUNK
    lane_iota = lax.broadcasted_iota(I32, (tq, LANES), 1)
    scale = (IDX_HEADS ** -0.5) * (HEAD_DIM ** -0.5)
    key_ninf = INT_MIN + 0x7FFFFF

    ikw = ikw_ref[...]
    for h in range(IDX_HEADS):
        wb_scr[h] = jnp.broadcast_to(ikw[:, HEAD_DIM + h:HEAD_DIM + h + 1] * scale, (tq, LANES))

    def score_tile(j, _):
        base = pl.multiple_of(j * tk, tk)
        ikt = ikt_ref[:, pl.ds(base, tk)]
        acc = [jnp.zeros((tq, LANES), F32) for _ in range(groups)]
        for h in range(IDX_HEADS):
            s = _dot(iq_ref[:, h * HEAD_DIM:(h + 1) * HEAD_DIM], ikt)
            w = wb_scr[h]
            for c in range(groups):
                acc[c] = acc[c] + w * jnp.maximum(s[:, c * LANES:(c + 1) * LANES], 0.0)
        for c in range(groups):
            kpos = base + c * LANES + lane_iota
            sc = jnp.where(acc[c] == 0.0, 0.0, acc[c])
            sc = jnp.where(kpos < vis_end, sc, -jnp.inf)
            key_scr[:, pl.ds(pl.multiple_of(base + c * LANES, LANES), LANES)] = _sortable(sc)
        return 0

    lax.fori_loop(0, n_tiles, score_tile, 0)

    def count(pred):
        def tile(j, acc):
            base = pl.multiple_of(j * tk, tk)
            for c in range(groups):
                off = pl.multiple_of(base + c * LANES, LANES)
                kv = key_scr[:, pl.ds(off, LANES)]
                acc = acc + jnp.where(pred(kv, off + lane_iota), 1, 0)
            return acc
        acc = lax.fori_loop(0, n_tiles, tile, jnp.zeros((tq, LANES), I32))
        return jnp.sum(acc, axis=-1, keepdims=True)

    def bcast(x):
        return jnp.broadcast_to(x, (tq, LANES))

    c0 = count(lambda kv, _: kv >= 0)
    prefix0 = bcast(jnp.where(c0 >= top_k, 0, INT_MIN))

    def bit_round(b, prefix):
        cand = prefix | (1 << (30 - b))
        cnt = count(lambda kv, _: kv >= cand)
        return jnp.where(bcast(cnt) >= top_k, cand, prefix)

    thr = lax.fori_loop(0, 31, bit_round, prefix0)

    c_gt = count(lambda kv, _: kv > thr)
    c_ge = count(lambda kv, _: kv >= thr)
    need = bcast(top_k - c_gt)
    j_scr[...] = jnp.full((tq, LANES), seq, I32)
    has_excess = jnp.max(jnp.where((c_ge > top_k) & (thr[:, 0:1] > key_ninf), 1, 0)) > 0

    @pl.when(has_excess)
    def _():
        def jbit(b, jp):
            cand = jp | (1 << (14 - b))
            cnt = count(lambda kv, pos: (kv == thr) & (pos < cand))
            return jnp.where(bcast(cnt) <= need, cand, jp)
        j_scr[...] = lax.fori_loop(0, 15, jbit, jnp.zeros((tq, LANES), I32))

    jlim = j_scr[...]

    def mask_tile(j, _):
        base = pl.multiple_of(j * tk, tk)
        for c in range(groups):
            off = pl.multiple_of(base + c * LANES, LANES)
            kv = key_scr[:, pl.ds(off, LANES)]
            pos = off + lane_iota
            sel = ((kv > thr) | ((kv == thr) & (pos < jlim))) & (kv > key_ninf)
            o_ref[:, pl.ds(off, LANES)] = jnp.where(sel, 1.0, 0.0).astype(o_ref.dtype)
        return 0

    lax.fori_loop(0, n_tiles, mask_tile, 0)

    def zero_tile(j, _):
        base = pl.multiple_of(j * tk, tk)
        o_ref[:, pl.ds(base, tk)] = jnp.zeros((tq, tk), o_ref.dtype)
        return 0

    lax.fori_loop(n_tiles, seq // tk, zero_tile, 0)


def _idx_topk(iq, ikw, ikt, *, top_k, tq=128, tk=512):
    seq = iq.shape[0]
    tq = min(tq, seq)
    tk = min(tk, seq)
    return pl.pallas_call(
        functools.partial(_idx_topk_body, tq=tq, tk=tk, top_k=top_k),
        grid=(seq // tq,),
        in_specs=[pl.BlockSpec((tq, IDX_HEADS * HEAD_DIM), lambda i: (i, 0)),
                  pl.BlockSpec((tq, LANES), lambda i: (i, 0)),
                  pl.BlockSpec((HEAD_DIM, seq), lambda i: (0, 0))],
        out_specs=pl.BlockSpec((tq, seq), lambda i: (i, 0)),
        out_shape=jax.ShapeDtypeStruct((seq, seq), BF16),
        scratch_shapes=[pltpu.VMEM((tq, seq), I32),
                        pltpu.VMEM((IDX_HEADS, tq, LANES), F32),
                        pltpu.VMEM((tq, LANES), I32)],
        compiler_params=_cparams("parallel"),
        name="idx_topk",
    )(iq, ikw, ikt)


def _attn_body(q_ref, k_ref, v_ref, m_ref, o_ref, m_scr, l_scr, acc_scr, *, tq, tk):
    i = pl.program_id(0)
    j = pl.program_id(1)
    last = ((i + 1) * tq - 1) // tk

    @pl.when(j == 0)
    def _():
        m_scr[...] = jnp.full_like(m_scr, NEG_BIG)
        l_scr[...] = jnp.zeros_like(l_scr)
        acc_scr[...] = jnp.zeros_like(acc_scr)

    @pl.when(j <= last)
    def _():
        sel = m_ref[...] > 0.5
        q = q_ref[...] * (HEAD_DIM ** -0.5)
        k = k_ref[...]
        v = v_ref[...]
        for h in range(ATT_HEADS):
            hs = slice(h * HEAD_DIM, (h + 1) * HEAD_DIM)
            s = jnp.where(sel, _dot_nt(q[:, hs], k[:, hs]), -jnp.inf)
            m_old = m_scr[h]
            m_new = jnp.maximum(m_old, jnp.max(s, axis=-1, keepdims=True))
            alpha = jnp.exp(m_old - m_new)
            p = jnp.exp(s - m_new)
            l_scr[h] = alpha * l_scr[h] + jnp.sum(p, axis=-1, keepdims=True)
            acc_scr[h] = alpha * acc_scr[h] + _dot(p.astype(BF16), v[:, hs])
            m_scr[h] = m_new

    @pl.when(j == pl.num_programs(1) - 1)
    def _():
        for h in range(ATT_HEADS):
            o_ref[:, h * HEAD_DIM:(h + 1) * HEAD_DIM] = (acc_scr[h] / l_scr[h]).astype(o_ref.dtype)


def _attention(qkv, mask, *, tq=256, tk=512):
    seq = qkv.shape[0]
    d = ATT_HEADS * HEAD_DIM
    tq = min(tq, seq)
    tk = min(tk, seq)

    def kv_block(i, j):
        return jnp.minimum(j, ((i + 1) * tq - 1) // tk)

    return pl.pallas_call(
        functools.partial(_attn_body, tq=tq, tk=tk),
        grid=(seq // tq, seq // tk),
        in_specs=[pl.BlockSpec((tq, d), lambda i, j: (i, 0)),
                  pl.BlockSpec((tk, d), lambda i, j: (kv_block(i, j), 1)),
                  pl.BlockSpec((tk, d), lambda i, j: (kv_block(i, j), 2)),
                  pl.BlockSpec((tq, tk), lambda i, j: (i, kv_block(i, j)))],
        out_specs=pl.BlockSpec((tq, d), lambda i, j: (i, 0)),
        out_shape=jax.ShapeDtypeStruct((seq, d), BF16),
        scratch_shapes=[pltpu.VMEM((ATT_HEADS, tq, 1), F32),
                        pltpu.VMEM((ATT_HEADS, tq, 1), F32),
                        pltpu.VMEM((ATT_HEADS, tq, HEAD_DIM), F32)],
        compiler_params=_cparams("parallel", "arbitrary"),
        name="dsa_attention",
    )(qkv, qkv, qkv, mask)


def _merge_body(x_ref, ya_ref, yb_ref, g_ref, woa_ref, wob_ref, wo_ref, o_ref):
    d = x_ref.shape[1]
    g = g_ref[...]
    merged = (jax.nn.sigmoid(g[:, 0:d]) * _dot(ya_ref[...], woa_ref[...])
              + jax.nn.sigmoid(g[:, d:2 * d]) * _dot(yb_ref[...], wob_ref[...]))
    o_ref[...] = x_ref[...] + _dot(merged.astype(BF16), wo_ref[...])


def _merge(x, ya, yb, gates, woa, wob, wo, *, tm=512):
    m, d = x.shape
    tm = min(tm, m)
    row = lambda n: pl.BlockSpec((tm, n), lambda i: (i, 0))
    wspec = pl.BlockSpec((d, d), lambda i: (0, 0))
    return pl.pallas_call(
        _merge_body,
        grid=(m // tm,),
        in_specs=[row(d), row(d), row(d), row(2 * d), wspec, wspec, wspec],
        out_specs=row(d),
        out_shape=jax.ShapeDtypeStruct((m, d), F32),
        compiler_params=_cparams("parallel"),
        name="merge",
    )(x, ya, yb, gates, woa, wob, wo)


def _layer(x, ffn1_norm, ffn1_w_gate, ffn1_w_up, ffn1_w_down, mix_norm, w_in, time_mix,
           decay_lora, decay_base, aaa_lora, aaa_base, gate_lora, k_k, k_a, r_k, gn_weight, gn_bias,
           w_out_rwkv, w_out_attn, w_out, ffn2_norm, ffn2_w_gate, ffn2_w_up, ffn2_w_down, out_norm,
           *, final):
    seq, d = x.shape
    bf = lambda w: w.astype(BF16)
    x1, hm = _ffn(x, ffn1_norm, bf(ffn1_w_gate), bf(ffn1_w_up), bf(ffn1_w_down), mix_norm, final=False)

    d_rwkv = RWKV_HEADS * HEAD_DIM
    d_att = ATT_HEADS * HEAD_DIM
    n_shift = 3 * d_rwkv + 2 * LANES
    o_qkv = n_shift
    o_iq = o_qkv + 3 * d_att
    o_ik = o_iq + IDX_HEADS * HEAD_DIM
    o_gate = o_ik + HEAD_DIM + IDX_HEADS
    w_ikw = jnp.pad(w_in[:, o_ik:o_gate], ((0, 0), (0, LANES - HEAD_DIM - IDX_HEADS)))
    p_rwkv = _matmul(hm, bf(w_in[:, :n_shift]), F32, tn=n_shift // 2, name="proj_rwkv")
    qkv = _matmul(hm, bf(w_in[:, o_qkv:o_iq]), BF16, tn=1024, name="proj_qkv")
    iq = _matmul(hm, bf(w_in[:, o_iq:o_ik]), BF16, name="proj_iq")
    ikw = _matmul(hm, bf(w_ikw), F32, name="proj_ikw")
    gates = _matmul(hm, bf(w_in[:, o_gate:]), F32, tn=1024, name="proj_gates")

    zpad = jnp.zeros((HEAD_DIM, d_rwkv), F32)
    wlora = bf(jnp.concatenate([decay_lora, zpad], axis=0))
    alora = bf(jnp.concatenate([zpad, aaa_lora], axis=0))
    ya = _rwkv(p_rwkv, time_mix, wlora, decay_base, alora, aaa_base, bf(gate_lora),
               k_k, k_a, r_k.reshape(-1), gn_weight, gn_bias)

    top_k = min(INDEX_TOPK, seq // 4)
    ikt = ikw[:, :HEAD_DIM].T.astype(BF16)
    mask = _idx_topk(iq, ikw, ikt, top_k=top_k)
    yb = _attention(qkv, mask)

    x2 = _merge(x1, ya, yb, gates, bf(w_out_rwkv), bf(w_out_attn), bf(w_out))
    return _ffn(x2, ffn2_norm, bf(ffn2_w_gate), bf(ffn2_w_up), bf(ffn2_w_down), out_norm, final=final)


def kernel(x, ffn1_norm, ffn1_w_gate, ffn1_w_up, ffn1_w_down, mix_norm, w_in, time_mix, decay_lora, decay_base, aaa_lora, aaa_base, gate_lora, k_k, k_a, r_k, gn_weight, gn_bias, w_out_rwkv, w_out_attn, w_out, ffn2_norm, ffn2_w_gate, ffn2_w_up, ffn2_w_down, final_norm):
    bsz, seq, d = x.shape
    depth = w_in.shape[0]
    assert bsz == 1 and depth == 1
    y = _layer(x[0], ffn1_norm[0], ffn1_w_gate[0], ffn1_w_up[0], ffn1_w_down[0], mix_norm[0], w_in[0],
               time_mix[0], decay_lora[0], decay_base[0], aaa_lora[0], aaa_base[0], gate_lora[0],
               k_k[0], k_a[0], r_k[0], gn_weight[0], gn_bias[0], w_out_rwkv[0], w_out_attn[0], w_out[0],
               ffn2_norm[0], ffn2_w_gate[0], ffn2_w_up[0], ffn2_w_down[0], final_norm, final=True)
    return y[None]
```

```python
import functools

import jax
import jax.numpy as jnp
from jax import lax
from jax.experimental import pallas as pl
from jax.experimental.pallas import tpu as pltpu

F32 = jnp.float32
BF16 = jnp.bfloat16
I32 = jnp.int32

NORM_EPS = 1e-6
CHUNK = 64
RWKV_HEADS = 16
HEAD_DIM = 64
GN_EPS = HEAD_DIM * 1e-5
ATT_HEADS = 16
IDX_HEADS = 8
INDEX_TOPK = 256
LANES = 128
SUBLANES = 8
VMEM_LIMIT = 56 * 1024 * 1024

INT_MIN = -2147483648
KEY_NEG_INF = INT_MIN + 0x7FFFFF
NEG_BIG = -1e30
MASK_DTYPE = jnp.int8


def _cparams(*sem):
    return pltpu.CompilerParams(dimension_semantics=sem, vmem_limit_bytes=VMEM_LIMIT)


def _rms(x, gain):
    return x * lax.rsqrt(jnp.mean(x * x, axis=-1, keepdims=True) + NORM_EPS) * gain


def _dot(a, b):
    return jnp.dot(a, b, preferred_element_type=F32)


def _dot_nt(a, b):
    return lax.dot_general(a, b, (((1,), (1,)), ((), ())), preferred_element_type=F32)


def _dot_tn(a, b):
    return lax.dot_general(a, b, (((0,), (0,)), ((), ())), preferred_element_type=F32)


def _ffn_body(x_ref, gin_ref, wg_ref, wu_ref, wd_ref, gout_ref, *rest, final):
    if final:
        o_ref, h_scr, acc_scr = rest
    else:
        xo_ref, ho_ref, h_scr, acc_scr = rest
    j = pl.program_id(1)

    @pl.when(j == 0)
    def _():
        h_scr[...] = _rms(x_ref[...], gin_ref[...]).astype(BF16)
        acc_scr[...] = jnp.zeros_like(acc_scr)

    h = h_scr[...]
    g = _dot(h, wg_ref[...])
    u = _dot(h, wu_ref[...])
    a = g * jax.nn.sigmoid(g) * u
    acc_scr[...] += _dot(a.astype(BF16), wd_ref[...])

    @pl.when(j == pl.num_programs(1) - 1)
    def _():
        xn = x_ref[...] + 0.5 * acc_scr[...]
        if final:
            o_ref[...] = _rms(xn, gout_ref[...])
        else:
            xo_ref[...] = xn
            ho_ref[...] = _rms(xn, gout_ref[...]).astype(BF16)


def _ffn(x, gin, wg, wu, wd, gout, *, final, tm=512, tf=1408):
    m, d = x.shape
    ff = wg.shape[1]
    tm = min(tm, m)
    grid = (m // tm, ff // tf)
    row = pl.BlockSpec((tm, d), lambda i, j: (i, 0))
    vec = pl.BlockSpec((1, d), lambda i, j: (0, 0))
    if final:
        out_shape = jax.ShapeDtypeStruct((m, d), F32)
        out_specs = row
    else:
        out_shape = (jax.ShapeDtypeStruct((m, d), F32), jax.ShapeDtypeStruct((m, d), BF16))
        out_specs = (row, row)
    return pl.pallas_call(
        functools.partial(_ffn_body, final=final),
        grid=grid,
        in_specs=[row, vec,
                  pl.BlockSpec((d, tf), lambda i, j: (0, j)),
                  pl.BlockSpec((d, tf), lambda i, j: (0, j)),
                  pl.BlockSpec((tf, d), lambda i, j: (j, 0)),
                  vec],
        out_specs=out_specs,
        out_shape=out_shape,
        scratch_shapes=[pltpu.VMEM((tm, d), BF16), pltpu.VMEM((tm, d), F32)],
        compiler_params=_cparams("parallel", "arbitrary"),
        name="ffn_final" if final else "ffn_mid",
    )(x, gin.reshape(1, d), wg, wu, wd, gout.reshape(1, d))


def _matmul_body(a_ref, b_ref, o_ref):
    o_ref[...] = _dot(a_ref[...], b_ref[...]).astype(o_ref.dtype)


def _matmul(a, b, out_dtype, *, tm=1024, tn=512, name="proj"):
    m, k = a.shape
    n = b.shape[1]
    tm = min(tm, m)
    tn = min(tn, n)
    return pl.pallas_call(
        _matmul_body,
        grid=(n // tn, m // tm),
        in_specs=[pl.BlockSpec((tm, k), lambda j, i: (i, 0)),
                  pl.BlockSpec((k, tn), lambda j, i: (0, j))],
        out_specs=pl.BlockSpec((tm, tn), lambda j, i: (i, j)),
        out_shape=jax.ShapeDtypeStruct((m, n), out_dtype),
        compiler_params=_cparams("parallel", "parallel"),
        name=name,
    )(a, b)


def _split3(x):
    h1 = x.astype(BF16)
    r1 = x - h1.astype(F32)
    h2 = r1.astype(BF16)
    h3 = (r1 - h2.astype(F32)).astype(BF16)
    return h1, h2, h3


def _head_sum(x, e_ref):
    h1, h2, h3 = _split3(x)
    e = e_ref[...]
    return _dot(h1, e) + _dot(h2, e) + _dot(h3, e)


def _rwkv_body(p_ref, mix_ref, wlora_ref, w0_ref, alora_ref, a0_ref, glora_ref,
               kk_ref, ka_ref, rk_ref, gnw_ref, gnb_ref, e_ref,
               o_ref,
               carry_scr, state_scr, r_scr, lw_scr, k_scr, v_scr, kk_scr, b_scr, y_scr,
               *, tb):
    d = RWKV_HEADS * HEAD_DIM
    t = CHUNK
    i = pl.program_id(0)

    @pl.when(i == 0)
    def _():
        carry_scr[...] = jnp.zeros_like(carry_scr)
        state_scr[...] = jnp.zeros_like(state_scr)

    p = p_ref[...]
    rows = lax.broadcasted_iota(I32, p.shape, 0)
    prev = jnp.where(rows == 0, carry_scr[...], pltpu.roll(p, 1, axis=0))
    carry_scr[...] = p[tb - 1:tb, :]
    ps = p + (prev - p) * mix_ref[...]
    r = ps[:, 0:d]
    k = ps[:, d:2 * d]
    v = ps[:, 2 * d:3 * d]
    xwa = ps[:, 3 * d:3 * d + LANES]
    xg = ps[:, 3 * d + LANES:3 * d + 2 * LANES]
    lane = lax.broadcasted_iota(I32, xwa.shape, 1)
    xwa = jnp.where(lane < 64, jnp.tanh(xwa), xwa).astype(BF16)
    zw = w0_ref[...] + _dot(xwa, wlora_ref[...])
    za = a0_ref[...] + _dot(xwa, alora_ref[...])
    sp = jnp.maximum(-zw, 0.0) + jnp.log(1.0 + jnp.exp(-jnp.abs(zw)))
    lw = -jnp.exp(-sp - 0.5)
    a = jax.nn.sigmoid(za)
    g = _dot(jax.nn.sigmoid(xg).astype(BF16), glora_ref[...])
    kk = k * kk_ref[...]
    nrm = jnp.maximum(jnp.sqrt(_head_sum(kk * kk, e_ref)), 1e-12)
    kk = kk / nrm
    k2 = k * (1.0 + (a - 1.0) * ka_ref[...])
    bonus = _head_sum(r * k2 * rk_ref[...], e_ref) * v
    r_scr[...] = r
    lw_scr[...] = lw
    k_scr[...] = k2
    v_scr[...] = v
    kk_scr[...] = kk
    b_scr[...] = kk * a

    ri = lax.broadcasted_iota(I32, (2 * t, t), 0)
    ci = lax.broadcasted_iota(I32, (2 * t, t), 1)
    tri_mask = ci < jnp.where(ri < t, ri, ri - t + 1)
    ri2 = lax.broadcasted_iota(I32, (t, t), 0)
    ci2 = lax.broadcasted_iota(I32, (t, t), 1)
    tri_incl = (ci2 <= ri2).astype(BF16)
    eye = (ci2 == ri2).astype(F32)

    def chunk_body(c, _):
        rs = pl.ds(pl.multiple_of(c * t, t), t)
        lwc = lw_scr[rs, :]
        l1, l2, l3 = _split3(lwc)
        gc = _dot(tri_incl, l1) + _dot(tri_incl, l2) + _dot(tri_incl, l3)
        gprev = gc - lwc
        gt = gc[t - 1:t, :]
        eneg = jnp.exp(-gc)
        eend = jnp.exp(gt - gc)
        kkc = kk_scr[rs, :]
        bc = b_scr[rs, :]
        kc = k_scr[rs, :]
        vc = v_scr[rs, :].astype(BF16)
        ar = jnp.concatenate([-kkc * jnp.exp(gprev), r_scr[rs, :] * jnp.exp(gc)], axis=0).astype(BF16)
        bt = (bc * eneg).astype(BF16)
        kt = (kc * eneg).astype(BF16)
        bk = jnp.concatenate([bc * eend, kc * eend], axis=0).astype(BF16)
        wt = jnp.exp(gt)
        heads = range(RWKV_HEADS)
        hsl = [slice(h * HEAD_DIM, (h + 1) * HEAD_DIM) for h in heads]
        ar_h = [ar[:, hs] for hs in hsl]
        v_h = [vc[:, hs] for hs in hsl]
        s_h = [state_scr[h] for h in heads]
        ab = [jnp.where(tri_mask, _dot_nt(ar_h[h], bt[:, hsl[h]]), 0.0) for h in heads]
        ak = [jnp.where(tri_mask, _dot_nt(ar_h[h], kt[:, hsl[h]]), 0.0) for h in heads]
        xy = [_dot_nt(ar_h[h], s_h[h].astype(BF16)) + _dot(ak[h].astype(BF16), v_h[h]) for h in heads]
        pw = [ab[h][0:t] for h in heads]
        tinv = [eye + pw[h] for h in heads]
        n = 1
        while 2 * n < t:
            pwb = [pw[h].astype(BF16) for h in heads]
            pw = [_dot(pwb[h], pwb[h]) for h in heads]
            tinv = [tinv[h] + _dot(tinv[h].astype(BF16), pw[h].astype(BF16)) for h in heads]
            n *= 2
        ub = [_dot(tinv[h].astype(BF16), xy[h][0:t].astype(BF16)).astype(BF16) for h in heads]
        y = [xy[h][t:2 * t] + _dot(ab[h][t:2 * t].astype(BF16), ub[h]) for h in heads]
        for h in heads:
            uv = jnp.concatenate([ub[h], v_h[h]], axis=0)
            state_scr[h] = s_h[h] * wt[:, hsl[h]] + _dot_tn(uv, bk[:, hsl[h]])
        for h in heads:
            mu = jnp.mean(y[h], axis=-1, keepdims=True)
            yc = y[h] - mu
            var = jnp.mean(yc * yc, axis=-1, keepdims=True)
            y_scr[rs, hsl[h]] = yc * lax.rsqrt(var + GN_EPS)
        return 0

    lax.fori_loop(0, tb // t, chunk_body, 0)

    o_ref[...] = ((y_scr[...] * gnw_ref[...] + gnb_ref[...] + bonus) * g).astype(o_ref.dtype)


def _rwkv(p_rwkv, mix, wlora, w0, alora, a0, glora, k_k, k_a, r_k, gnw, gnb, *, tb=256):
    m, dp = p_rwkv.shape
    d = RWKV_HEADS * HEAD_DIM
    tb = min(tb, m)
    head_of = jnp.arange(d) // HEAD_DIM
    e = (head_of[:, None] == head_of[None, :]).astype(BF16)
    vec = lambda n: pl.BlockSpec((1, n), lambda i: (0, 0))
    full = lambda a, b: pl.BlockSpec((a, b), lambda i: (0, 0))
    return pl.pallas_call(
        functools.partial(_rwkv_body, tb=tb),
        grid=(m // tb,),
        in_specs=[pl.BlockSpec((tb, dp), lambda i: (i, 0)), vec(dp),
                  full(LANES, d), vec(d), full(LANES, d), vec(d), full(LANES, d),
                  vec(d), vec(d), vec(d), vec(d), vec(d), full(d, d)],
        out_specs=pl.BlockSpec((tb, d), lambda i: (i, 0)),
        out_shape=jax.ShapeDtypeStruct((m, d), BF16),
        scratch_shapes=[pltpu.VMEM((1, dp), F32),
                        pltpu.VMEM((RWKV_HEADS, HEAD_DIM, HEAD_DIM), F32)]
                       + [pltpu.VMEM((tb, d), F32)] * 7,
        compiler_params=_cparams("arbitrary"),
        name="rwkv7",
    )(p_rwkv, mix.reshape(1, dp), wlora, w0.reshape(1, d), alora, a0.reshape(1, d), glora,
      k_k.reshape(1, d), k_a.reshape(1, d), r_k.reshape(1, d), gnw.reshape(1, d), gnb.reshape(1, d), e)


def _rank_to_f32(k):
    return pltpu.bitcast(k ^ ((k >> 31) & 0x7FFFFFFF), F32)


def _idx_topk_body(ik_ref, iq_ref, iwt_ref, o_ref, sc_scr, j_scr, *, tq, tk, top_k):
    seq = o_ref.shape[0]
    qb = pl.program_id(0)
    n_tiles = ((qb + 1) * tq + tk - 1) // tk
    q_pos = qb * tq + lax.broadcasted_iota(I32, (1, tq), 1)
    vis_end = (q_pos // CHUNK + 1) * CHUNK
    row_iota = lax.broadcasted_iota(I32, (tk, tq), 0)
    scale = (IDX_HEADS ** -0.5) * (HEAD_DIM ** -0.5)
    iq = iq_ref[...]
    iq_h = [iq[:, h * HEAD_DIM:(h + 1) * HEAD_DIM] for h in range(IDX_HEADS)]
    w_h = [iwt_ref[h:h + 1, :] * scale for h in range(IDX_HEADS)]

    def tile_rows(j):
        return pl.ds(pl.multiple_of(j * tk, tk), tk)

    def score_tile(j, _):
        ik = ik_ref[tile_rows(j), :]
        acc = jnp.zeros((tk, tq), F32)
        for h in range(IDX_HEADS):
            acc = acc + w_h[h] * jnp.maximum(_dot_nt(ik, iq_h[h]), 0.0)
        acc = jnp.where(acc == 0.0, 0.0, acc)
        sc_scr[tile_rows(j), :] = jnp.where(j * tk + row_iota < vis_end, acc, -jnp.inf)
        return 0

    lax.fori_loop(0, n_tiles, score_tile, 0)

    def count(pred):
        def tile(j, acc):
            hit = jnp.where(pred(sc_scr[tile_rows(j), :], j * tk + row_iota), 1, 0)
            return acc + jnp.sum(hit.reshape(tk // SUBLANES, SUBLANES, tq), axis=0)
        acc = lax.fori_loop(0, n_tiles, tile, jnp.zeros((SUBLANES, tq), I32))
        return jnp.sum(acc, axis=0, keepdims=True)

    n_finite = count(lambda sc, _: sc > -jnp.inf)
    c0 = count(lambda sc, _: sc >= 0.0)
    prefix0 = jnp.where(c0 >= top_k, 0, INT_MIN)

    def bit_round(b, prefix):
        cand = prefix | (1 << (30 - b))
        cand_f = _rank_to_f32(cand)
        cnt = count(lambda sc, _: sc >= cand_f)
        return jnp.where(cnt >= top_k, cand, prefix)

    thr_key = lax.fori_loop(0, 31, bit_round, prefix0)
    thr = _rank_to_f32(jnp.where(n_finite < top_k, KEY_NEG_INF, thr_key))

    c_gt = count(lambda sc, _: sc > thr)
    c_ge = count(lambda sc, _: sc >= thr)
    need = top_k - c_gt
    j_scr[...] = jnp.full((1, tq), seq, I32)
    has_excess = jnp.max(jnp.where((c_ge > top_k) & (n_finite >= top_k), 1, 0)) > 0

    @pl.when(has_excess)
    def _():
        nbits = seq.bit_length()

        def jbit(b, jp):
            cand = jp | (1 << (nbits - 1 - b))
            cnt = count(lambda sc, pos: (sc == thr) & (pos < cand))
            return jnp.where(cnt <= need, cand, jp)

        j_scr[...] = lax.fori_loop(0, nbits, jbit, jnp.zeros((1, tq), I32))

    jlim = j_scr[...]

    def mask_tile(j, _):
        sc = sc_scr[tile_rows(j), :]
        sel = ((sc > thr) | ((sc == thr) & (j * tk + row_iota < jlim))) & (sc > -jnp.inf)
        o_ref[tile_rows(j), :] = jnp.where(sel, 1, 0).astype(o_ref.dtype)
        return 0

    lax.fori_loop(0, n_tiles, mask_tile, 0)

    def zero_tile(j, _):
        o_ref[tile_rows(j), :] = jnp.zeros((tk, tq), o_ref.dtype)
        return 0

    lax.fori_loop(n_tiles, seq // tk, zero_tile, 0)


def _idx_topk(ik, iq, iwt, *, top_k, tq=256, tk=512):
    seq = iq.shape[0]
    tq = min(tq, seq)
    tk = min(tk, seq)
    return pl.pallas_call(
        functools.partial(_idx_topk_body, tq=tq, tk=tk, top_k=top_k),
        grid=(seq // tq,),
        in_specs=[pl.BlockSpec((seq, HEAD_DIM), lambda i: (0, 0)),
                  pl.BlockSpec((tq, IDX_HEADS * HEAD_DIM), lambda i: (i, 0)),
                  pl.BlockSpec((IDX_HEADS, tq), lambda i: (0, i))],
        out_specs=pl.BlockSpec((seq, tq), lambda i: (0, i)),
        out_shape=jax.ShapeDtypeStruct((seq, seq), MASK_DTYPE),
        scratch_shapes=[pltpu.VMEM((seq, tq), F32),
                        pltpu.VMEM((1, tq), I32)],
        compiler_params=_cparams("parallel"),
        name="idx_topk",
    )(ik, iq, iwt)


VT_ROWS = HEAD_DIM + 16


def _attn_body(q_ref, k_ref, vt_ref, m_ref, o_ref, m_scr, l_scr, acc_scr, *, tq, tk, ahead):
    i = pl.program_id(0)
    j = pl.program_id(1)
    last = ((i + 1) * tq - 1) // tk

    @pl.when(j == 0)
    def _():
        m_scr[...] = jnp.full_like(m_scr, NEG_BIG)
        l_scr[...] = jnp.zeros_like(l_scr)
        acc_scr[...] = jnp.zeros_like(acc_scr)

    @pl.when(j <= last)
    def _():
        cap = jnp.where(m_ref[...].astype(I32) != 0, jnp.inf, -jnp.inf)
        q = q_ref[...] * (HEAD_DIM ** -0.5)
        k = k_ref[...]
        hsl = [slice(h * HEAD_DIM, (h + 1) * HEAD_DIM) for h in range(ATT_HEADS)]

        def scores(h):
            return _dot_nt(k[:, hsl[h]], q[:, hsl[h]])

        s_raw = [scores(h) for h in range(ahead)]
        for h in range(ATT_HEADS):
            if h + ahead < ATT_HEADS:
                s_raw.append(scores(h + ahead))
            s = jnp.minimum(s_raw[h], cap)
            m_old = m_scr[h:h + 1, :]
            m_new = jnp.maximum(m_old, jnp.max(s, axis=0, keepdims=True))
            alpha = jnp.exp(m_old - m_new)
            p = jnp.exp(s - m_new).astype(BF16)
            pv = _dot(vt_ref[h * VT_ROWS:(h + 1) * VT_ROWS, :], p)
            l_scr[h:h + 1, :] = alpha * l_scr[h:h + 1, :] + pv[HEAD_DIM:HEAD_DIM + 1]
            acc_scr[h] = alpha * acc_scr[h] + pv[0:HEAD_DIM]
            m_scr[h:h + 1, :] = m_new

    @pl.when(j == pl.num_programs(1) - 1)
    def _():
        for h in range(ATT_HEADS):
            out_t = acc_scr[h] / l_scr[h:h + 1, :]
            o_ref[:, h * HEAD_DIM:(h + 1) * HEAD_DIM] = out_t.T.astype(o_ref.dtype)


def _attention(qkv, vt, mask_t, *, tq=256, tk=512, ahead=4):
    seq = qkv.shape[0]
    d = ATT_HEADS * HEAD_DIM
    tq = min(tq, seq)
    tk = min(tk, seq)

    def kv_block(i, j):
        return jnp.minimum(j, ((i + 1) * tq - 1) // tk)

    return pl.pallas_call(
        functools.partial(_attn_body, tq=tq, tk=tk, ahead=ahead),
        grid=(seq // tq, seq // tk),
        in_specs=[pl.BlockSpec((tq, d), lambda i, j: (i, 0)),
                  pl.BlockSpec((tk, d), lambda i, j: (kv_block(i, j), 1)),
                  pl.BlockSpec((ATT_HEADS * VT_ROWS, tk), lambda i, j: (0, kv_block(i, j))),
                  pl.BlockSpec((tk, tq), lambda i, j: (kv_block(i, j), i))],
        out_specs=pl.BlockSpec((tq, d), lambda i, j: (i, 0)),
        out_shape=jax.ShapeDtypeStruct((seq, d), BF16),
        scratch_shapes=[pltpu.VMEM((ATT_HEADS, tq), F32),
                        pltpu.VMEM((ATT_HEADS, tq), F32),
                        pltpu.VMEM((ATT_HEADS, HEAD_DIM, tq), F32)],
        compiler_params=_cparams("parallel", "arbitrary"),
        name="dsa_attention",
    )(qkv, qkv, vt, mask_t)


def _merge_body(x_ref, ya_ref, yb_ref, g_ref, woa_ref, wob_ref, wo_ref, o_ref):
    d = x_ref.shape[1]
    g = g_ref[...]
    merged = (jax.nn.sigmoid(g[:, 0:d]) * _dot(ya_ref[...], woa_ref[...])
              + jax.nn.sigmoid(g[:, d:2 * d]) * _dot(yb_ref[...], wob_ref[...]))
    o_ref[...] = x_ref[...] + _dot(merged.astype(BF16), wo_ref[...])


def _merge(x, ya, yb, gates, woa, wob, wo, *, tm=512):
    m, d = x.shape
    tm = min(tm, m)
    row = lambda n: pl.BlockSpec((tm, n), lambda i: (i, 0))
    wspec = pl.BlockSpec((d, d), lambda i: (0, 0))
    return pl.pallas_call(
        _merge_body,
        grid=(m // tm,),
        in_specs=[row(d), row(d), row(d), row(2 * d), wspec, wspec, wspec],
        out_specs=row(d),
        out_shape=jax.ShapeDtypeStruct((m, d), F32),
        compiler_params=_cparams("parallel"),
        name="merge",
    )(x, ya, yb, gates, woa, wob, wo)


def _layer(x, ffn1_norm, ffn1_w_gate, ffn1_w_up, ffn1_w_down, mix_norm, w_in, time_mix,
           decay_lora, decay_base, aaa_lora, aaa_base, gate_lora, k_k, k_a, r_k, gn_weight, gn_bias,
           w_out_rwkv, w_out_attn, w_out, ffn2_norm, ffn2_w_gate, ffn2_w_up, ffn2_w_down, out_norm,
           *, final):
    seq, d = x.shape
    bf = lambda w: w.astype(BF16)
    x1, hm = _ffn(x, ffn1_norm, bf(ffn1_w_gate), bf(ffn1_w_up), bf(ffn1_w_down), mix_norm, final=False)

    d_rwkv = RWKV_HEADS * HEAD_DIM
    d_att = ATT_HEADS * HEAD_DIM
    n_shift = 3 * d_rwkv + 2 * LANES
    o_qkv = n_shift
    o_iq = o_qkv + 3 * d_att
    o_ik = o_iq + IDX_HEADS * HEAD_DIM
    o_gate = o_ik + HEAD_DIM + IDX_HEADS
    w_ikw = jnp.pad(w_in[:, o_ik:o_gate], ((0, 0), (0, LANES - HEAD_DIM - IDX_HEADS)))
    p_rwkv = _matmul(hm, bf(w_in[:, :n_shift]), F32, tn=n_shift // 2, name="proj_rwkv")
    qkv = _matmul(hm, bf(w_in[:, o_qkv:o_iq]), BF16, tn=1024, name="proj_qkv")
    iq = _matmul(hm, bf(w_in[:, o_iq:o_ik]), BF16, name="proj_iq")
    ikw = _matmul(hm, bf(w_ikw), F32, name="proj_ikw")
    gates = _matmul(hm, bf(w_in[:, o_gate:]), F32, tn=1024, name="proj_gates")

    zpad = jnp.zeros((HEAD_DIM, d_rwkv), F32)
    wlora = bf(jnp.concatenate([decay_lora, zpad], axis=0))
    alora = bf(jnp.concatenate([zpad, aaa_lora], axis=0))
    ya = _rwkv(p_rwkv, time_mix, wlora, decay_base, alora, aaa_base, bf(gate_lora),
               k_k, k_a, r_k.reshape(-1), gn_weight, gn_bias)

    top_k = min(INDEX_TOPK, seq // 4)
    ik = bf(ikw[:, :HEAD_DIM])
    iwt = ikw[:, HEAD_DIM:HEAD_DIM + IDX_HEADS].T
    mask_t = _idx_topk(ik, iq, iwt, top_k=top_k)
    vt = qkv[:, 2 * d_att:].T.reshape(ATT_HEADS, HEAD_DIM, seq)
    ones = jnp.ones((ATT_HEADS, VT_ROWS - HEAD_DIM, seq), BF16)
    vt_aug = jnp.concatenate([vt, ones], axis=1).reshape(ATT_HEADS * VT_ROWS, seq)
    yb = _attention(qkv, vt_aug, mask_t)

    x2 = _merge(x1, ya, yb, gates, bf(w_out_rwkv), bf(w_out_attn), bf(w_out))
    return _ffn(x2, ffn2_norm, bf(ffn2_w_gate), bf(ffn2_w_up), bf(ffn2_w_down), out_norm, final=final)


def kernel(x, ffn1_norm, ffn1_w_gate, ffn1_w_up, ffn1_w_down, mix_norm, w_in, time_mix, decay_lora, decay_base, aaa_lora, aaa_base, gate_lora, k_k, k_a, r_k, gn_weight, gn_bias, w_out_rwkv, w_out_attn, w_out, ffn2_norm, ffn2_w_gate, ffn2_w_up, ffn2_w_down, final_norm):
    bsz, seq, d = x.shape
    depth = w_in.shape[0]
    assert bsz == 1 and depth == 1
    y = _layer(x[0], ffn1_norm[0], ffn1_w_gate[0], ffn1_w_up[0], ffn1_w_down[0], mix_norm[0], w_in[0],
               time_mix[0], decay_lora[0], decay_base[0], aaa_lora[0], aaa_base[0], gate_lora[0],
               k_k[0], k_a[0], r_k[0], gn_weight[0], gn_bias[0], w_out_rwkv[0], w_out_attn[0], w_out[0],
               ffn2_norm[0], ffn2_w_gate[0], ffn2_w_up[0], ffn2_w_down[0], final_norm, final=True)
    return y[None]
```

```python
import functools

import jax
import jax.numpy as jnp
from jax import lax
from jax.experimental import pallas as pl
from jax.experimental.pallas import tpu as pltpu

F32 = jnp.float32
BF16 = jnp.bfloat16
I32 = jnp.int32
I16 = jnp.int16

NORM_EPS = 1e-6
CHUNK = 64
RWKV_HEADS = 16
HEAD_DIM = 64
GN_EPS = HEAD_DIM * 1e-5
ATT_HEADS = 16
IDX_HEADS = 8
INDEX_TOPK = 256
LANES = 128
SUBLANES = 8
VMEM_LIMIT = 56 * 1024 * 1024

INT_MIN = -2147483648
KEY_NEG_INF = INT_MIN + 0x7FFFFF
HI16 = -0x10000
NEG_BIG = -1e30
MASK_DTYPE = jnp.int8


def _cparams(*sem):
    return pltpu.CompilerParams(dimension_semantics=sem, vmem_limit_bytes=VMEM_LIMIT)


def _rms(x, gain):
    return x * lax.rsqrt(jnp.mean(x * x, axis=-1, keepdims=True) + NORM_EPS) * gain


def _dot(a, b):
    return jnp.dot(a, b, preferred_element_type=F32)


def _dot_nt(a, b):
    return lax.dot_general(a, b, (((1,), (1,)), ((), ())), preferred_element_type=F32)


def _dot_tn(a, b):
    return lax.dot_general(a, b, (((0,), (0,)), ((), ())), preferred_element_type=F32)


def _ffn_body(x_ref, gin_ref, wg_ref, wu_ref, wd_ref, gout_ref, *rest, final):
    if final:
        o_ref, h_scr, acc_scr = rest
    else:
        xo_ref, ho_ref, h_scr, acc_scr = rest
    j = pl.program_id(1)

    @pl.when(j == 0)
    def _():
        h_scr[...] = _rms(x_ref[...], gin_ref[...]).astype(BF16)
        acc_scr[...] = jnp.zeros_like(acc_scr)

    h = h_scr[...]
    g = _dot(h, wg_ref[...])
    u = _dot(h, wu_ref[...])
    a = g * jax.nn.sigmoid(g) * u
    acc_scr[...] += _dot(a.astype(BF16), wd_ref[...])

    @pl.when(j == pl.num_programs(1) - 1)
    def _():
        xn = x_ref[...] + 0.5 * acc_scr[...]
        if final:
            o_ref[...] = _rms(xn, gout_ref[...])
        else:
            xo_ref[...] = xn
            ho_ref[...] = _rms(xn, gout_ref[...]).astype(BF16)


def _ffn(x, gin, wg, wu, wd, gout, *, final, tm=512, tf=2816):
    m, d = x.shape
    ff = wg.shape[1]
    tm = min(tm, m)
    grid = (m // tm, ff // tf)
    row = pl.BlockSpec((tm, d), lambda i, j: (i, 0))
    vec = pl.BlockSpec((1, d), lambda i, j: (0, 0))
    if final:
        out_shape = jax.ShapeDtypeStruct((m, d), F32)
        out_specs = row
    else:
        out_shape = (jax.ShapeDtypeStruct((m, d), F32), jax.ShapeDtypeStruct((m, d), BF16))
        out_specs = (row, row)
    return pl.pallas_call(
        functools.partial(_ffn_body, final=final),
        grid=grid,
        in_specs=[row, vec,
                  pl.BlockSpec((d, tf), lambda i, j: (0, j)),
                  pl.BlockSpec((d, tf), lambda i, j: (0, j)),
                  pl.BlockSpec((tf, d), lambda i, j: (j, 0)),
                  vec],
        out_specs=out_specs,
        out_shape=out_shape,
        scratch_shapes=[pltpu.VMEM((tm, d), BF16), pltpu.VMEM((tm, d), F32)],
        compiler_params=_cparams("parallel", "arbitrary"),
        name="ffn_final" if final else "ffn_mid",
    )(x, gin.reshape(1, d), wg, wu, wd, gout.reshape(1, d))


def _matmul_body(a_ref, b_ref, o_ref):
    o_ref[...] = _dot(a_ref[...], b_ref[...]).astype(o_ref.dtype)


def _matmul(a, b, out_dtype, *, tm=1024, tn=512, name="proj"):
    m, k = a.shape
    n = b.shape[1]
    tm = min(tm, m)
    tn = min(tn, n)
    return pl.pallas_call(
        _matmul_body,
        grid=(n // tn, m // tm),
        in_specs=[pl.BlockSpec((tm, k), lambda j, i: (i, 0)),
                  pl.BlockSpec((k, tn), lambda j, i: (0, j))],
        out_specs=pl.BlockSpec((tm, tn), lambda j, i: (i, j)),
        out_shape=jax.ShapeDtypeStruct((m, n), out_dtype),
        compiler_params=_cparams("parallel", "parallel"),
        name=name,
    )(a, b)


def _split3(x):
    h1 = x.astype(BF16)
    r1 = x - h1.astype(F32)
    h2 = r1.astype(BF16)
    h3 = (r1 - h2.astype(F32)).astype(BF16)
    return h1, h2, h3


def _head_sum(x, e_ref):
    h1, h2, h3 = _split3(x)
    e = e_ref[...]
    return _dot(h1, e) + _dot(h2, e) + _dot(h3, e)


def _rwkv_body(p_ref, mix_ref, wlora_ref, w0_ref, alora_ref, a0_ref, glora_ref,
               kk_ref, ka_ref, rk_ref, gnw_ref, gnb_ref, e_ref,
               o_ref,
               carry_scr, state_scr, r_scr, lw_scr, k_scr, v_scr, kk_scr, b_scr, y_scr,
               *, tb):
    d = RWKV_HEADS * HEAD_DIM
    t = CHUNK
    i = pl.program_id(0)

    @pl.when(i == 0)
    def _():
        carry_scr[...] = jnp.zeros_like(carry_scr)
        state_scr[...] = jnp.zeros_like(state_scr)

    p = p_ref[...]
    rows = lax.broadcasted_iota(I32, p.shape, 0)
    prev = jnp.where(rows == 0, carry_scr[...], pltpu.roll(p, 1, axis=0))
    carry_scr[...] = p[tb - 1:tb, :]
    ps = p + (prev - p) * mix_ref[...]
    r = ps[:, 0:d]
    k = ps[:, d:2 * d]
    v = ps[:, 2 * d:3 * d]
    xwa = ps[:, 3 * d:3 * d + LANES]
    xg = ps[:, 3 * d + LANES:3 * d + 2 * LANES]
    lane = lax.broadcasted_iota(I32, xwa.shape, 1)
    xwa = jnp.where(lane < 64, jnp.tanh(xwa), xwa).astype(BF16)
    zw = w0_ref[...] + _dot(xwa, wlora_ref[...])
    za = a0_ref[...] + _dot(xwa, alora_ref[...])
    sp = jnp.maximum(-zw, 0.0) + jnp.log(1.0 + jnp.exp(-jnp.abs(zw)))
    lw = -jnp.exp(-sp - 0.5)
    a = jax.nn.sigmoid(za)
    g = _dot(jax.nn.sigmoid(xg).astype(BF16), glora_ref[...])
    kk = k * kk_ref[...]
    nrm = jnp.maximum(jnp.sqrt(_head_sum(kk * kk, e_ref)), 1e-12)
    kk = kk / nrm
    k2 = k * (1.0 + (a - 1.0) * ka_ref[...])
    bonus = _head_sum(r * k2 * rk_ref[...], e_ref) * v
    r_scr[...] = r
    lw_scr[...] = lw
    k_scr[...] = k2
    v_scr[...] = v
    kk_scr[...] = kk
    b_scr[...] = kk * a

    ri = lax.broadcasted_iota(I32, (2 * t, t), 0)
    ci = lax.broadcasted_iota(I32, (2 * t, t), 1)
    tri_mask = ci < jnp.where(ri < t, ri, ri - t + 1)
    ri2 = lax.broadcasted_iota(I32, (t, t), 0)
    ci2 = lax.broadcasted_iota(I32, (t, t), 1)
    tri_incl = (ci2 <= ri2).astype(BF16)
    eye = (ci2 == ri2).astype(F32)

    def chunk_body(c, _):
        rs = pl.ds(pl.multiple_of(c * t, t), t)
        lwc = lw_scr[rs, :]
        l1, l2, l3 = _split3(lwc)
        gc = _dot(tri_incl, l1) + _dot(tri_incl, l2) + _dot(tri_incl, l3)
        gprev = gc - lwc
        gt = gc[t - 1:t, :]
        eneg = jnp.exp(-gc)
        eend = jnp.exp(gt - gc)
        kkc = kk_scr[rs, :]
        bc = b_scr[rs, :]
        kc = k_scr[rs, :]
        vc = v_scr[rs, :].astype(BF16)
        ar = jnp.concatenate([-kkc * jnp.exp(gprev), r_scr[rs, :] * jnp.exp(gc)], axis=0).astype(BF16)
        bt = (bc * eneg).astype(BF16)
        kt = (kc * eneg).astype(BF16)
        bk = jnp.concatenate([bc * eend, kc * eend], axis=0).astype(BF16)
        wt = jnp.exp(gt)
        heads = range(RWKV_HEADS)
        hsl = [slice(h * HEAD_DIM, (h + 1) * HEAD_DIM) for h in heads]
        ar_h = [ar[:, hs] for hs in hsl]
        v_h = [vc[:, hs] for hs in hsl]
        s_h = [state_scr[h] for h in heads]
        ab = [jnp.where(tri_mask, _dot_nt(ar_h[h], bt[:, hsl[h]]), 0.0) for h in heads]
        ak = [jnp.where(tri_mask, _dot_nt(ar_h[h], kt[:, hsl[h]]), 0.0) for h in heads]
        xy = [_dot_nt(ar_h[h], s_h[h].astype(BF16)) + _dot(ak[h].astype(BF16), v_h[h]) for h in heads]
        pw = [ab[h][0:t] for h in heads]
        tinv = [eye + pw[h] for h in heads]
        n = 1
        while 2 * n < t:
            pwb = [pw[h].astype(BF16) for h in heads]
            pw = [_dot(pwb[h], pwb[h]) for h in heads]
            tinv = [tinv[h] + _dot(tinv[h].astype(BF16), pw[h].astype(BF16)) for h in heads]
            n *= 2
        ub = [_dot(tinv[h].astype(BF16), xy[h][0:t].astype(BF16)).astype(BF16) for h in heads]
        y = [xy[h][t:2 * t] + _dot(ab[h][t:2 * t].astype(BF16), ub[h]) for h in heads]
        for h in heads:
            uv = jnp.concatenate([ub[h], v_h[h]], axis=0)
            state_scr[h] = s_h[h] * wt[:, hsl[h]] + _dot_tn(uv, bk[:, hsl[h]])
        for h in heads:
            mu = jnp.mean(y[h], axis=-1, keepdims=True)
            yc = y[h] - mu
            var = jnp.mean(yc * yc, axis=-1, keepdims=True)
            y_scr[rs, hsl[h]] = yc * lax.rsqrt(var + GN_EPS)
        return 0

    lax.fori_loop(0, tb // t, chunk_body, 0)

    o_ref[...] = ((y_scr[...] * gnw_ref[...] + gnb_ref[...] + bonus) * g).astype(o_ref.dtype)


def _rwkv(p_rwkv, mix, wlora, w0, alora, a0, glora, k_k, k_a, r_k, gnw, gnb, *, tb=256):
    m, dp = p_rwkv.shape
    d = RWKV_HEADS * HEAD_DIM
    tb = min(tb, m)
    head_of = jnp.arange(d) // HEAD_DIM
    e = (head_of[:, None] == head_of[None, :]).astype(BF16)
    vec = lambda n: pl.BlockSpec((1, n), lambda i: (0, 0))
    full = lambda a, b: pl.BlockSpec((a, b), lambda i: (0, 0))
    return pl.pallas_call(
        functools.partial(_rwkv_body, tb=tb),
        grid=(m // tb,),
        in_specs=[pl.BlockSpec((tb, dp), lambda i: (i, 0)), vec(dp),
                  full(LANES, d), vec(d), full(LANES, d), vec(d), full(LANES, d),
                  vec(d), vec(d), vec(d), vec(d), vec(d), full(d, d)],
        out_specs=pl.BlockSpec((tb, d), lambda i: (i, 0)),
        out_shape=jax.ShapeDtypeStruct((m, d), BF16),
        scratch_shapes=[pltpu.VMEM((1, dp), F32),
                        pltpu.VMEM((RWKV_HEADS, HEAD_DIM, HEAD_DIM), F32)]
                       + [pltpu.VMEM((tb, d), F32)] * 7,
        compiler_params=_cparams("arbitrary"),
        name="rwkv7",
    )(p_rwkv, mix.reshape(1, dp), wlora, w0.reshape(1, d), alora, a0.reshape(1, d), glora,
      k_k.reshape(1, d), k_a.reshape(1, d), r_k.reshape(1, d), gnw.reshape(1, d), gnb.reshape(1, d), e)


def _rank_to_f32(k):
    return pltpu.bitcast(k ^ ((k >> 31) & 0x7FFFFFFF), F32)


def _idx_topk_body(ik_ref, iqt_ref, iwt_ref, o_ref, sc_scr, hi_scr, lo_scr, j_scr, *, tq, tk, top_k, acc_rows, slab):
    seq = o_ref.shape[0]
    qb = pl.program_id(0)
    n_tiles = ((qb + 1) * tq + tk - 1) // tk
    q_pos = qb * tq + lax.broadcasted_iota(I32, (1, tq), 1)
    vis_end = (q_pos // CHUNK + 1) * CHUNK
    row_iota = lax.broadcasted_iota(I32, (tk, tq), 0)
    scale = (IDX_HEADS ** -0.5) * (HEAD_DIM ** -0.5)
    iq_h = [iqt_ref[h * HEAD_DIM:(h + 1) * HEAD_DIM, :] for h in range(IDX_HEADS)]
    w_h = [iwt_ref[h:h + 1, :] * scale for h in range(IDX_HEADS)]

    def tile_rows(j):
        return pl.ds(pl.multiple_of(j * tk, tk), tk)

    def score_tile(j, _):
        ik = ik_ref[tile_rows(j), :]
        acc = jnp.zeros((tk, tq), F32)
        for h in range(IDX_HEADS):
            acc = acc + w_h[h] * jnp.maximum(_dot(ik, iq_h[h]), 0.0)
        acc = jnp.where(acc == 0.0, 0.0, acc)
        sc = jnp.where(j * tk + row_iota < vis_end, acc, -jnp.inf)
        sc_scr[tile_rows(j), :] = sc
        hi_scr[tile_rows(j), :] = pltpu.bitcast(pltpu.bitcast(sc, I32) & HI16, F32).astype(BF16)
        return 0

    lax.fori_loop(0, n_tiles, score_tile, 0)

    def count16(ref, pred):
        def tile(j, acc):
            hit = jnp.where(pred(ref[tile_rows(j), :]), jnp.ones((), I16), jnp.zeros((), I16))
            for r in range(tk // slab):
                acc = acc + hit[r * slab:(r + 1) * slab]
            return acc
        acc = lax.fori_loop(0, n_tiles, tile, jnp.zeros((slab, tq), I16))
        return jnp.sum(acc.astype(I32), axis=0, keepdims=True)

    def count(*preds):
        def tile(j, accs):
            sc = sc_scr[tile_rows(j), :]
            pos = j * tk + row_iota
            return tuple(a + jnp.sum(jnp.where(p(sc, pos), 1, 0).reshape(tk // acc_rows, acc_rows, tq), axis=0)
                         for a, p in zip(accs, preds))
        accs = lax.fori_loop(0, n_tiles, tile, tuple(jnp.zeros((acc_rows, tq), I32) for _ in preds))
        outs = tuple(jnp.sum(a, axis=0, keepdims=True) for a in accs)
        return outs if len(outs) > 1 else outs[0]

    n_finite = count16(hi_scr, lambda x: x > -jnp.inf)
    c0 = count16(hi_scr, lambda x: x >= 0.0)
    prefix0 = jnp.where(c0 >= top_k, 0, INT_MIN)

    def hi_candidate(key):
        return pltpu.bitcast((key ^ ((key >> 31) & 0x7FFFFFFF)) & HI16, F32).astype(BF16)

    def hi_round(b, prefix):
        cand = prefix | (1 << (30 - b))
        cand_b = hi_candidate(cand)
        cnt = count16(hi_scr, lambda x: x >= cand_b)
        return jnp.where(cnt >= top_k, cand, prefix)

    prefix_hi = lax.fori_loop(0, 15, hi_round, prefix0)
    above_b = hi_candidate(prefix_hi + 0x10000)
    c_above = count16(hi_scr, lambda x: x >= above_b)

    def lo_tile(j, _):
        b = pltpu.bitcast(sc_scr[tile_rows(j), :], I32)
        d = (b ^ ((b >> 31) & 0x7FFFFFFF)) - prefix_hi
        lo_scr[tile_rows(j), :] = jnp.where((d & HI16) == 0, d - 0x8000, -0x8000).astype(I16)
        return 0

    lax.fori_loop(0, n_tiles, lo_tile, 0)

    def lo_round(b, plo):
        cand = plo | (1 << (15 - b))
        cand_s = (cand - 0x8000).astype(I16)
        cnt = c_above + count16(lo_scr, lambda x: x >= cand_s)
        return jnp.where(cnt >= top_k, cand, plo)

    thr_key = prefix_hi | lax.fori_loop(0, 16, lo_round, jnp.zeros((1, tq), I32))
    thr = _rank_to_f32(jnp.where(n_finite < top_k, KEY_NEG_INF, thr_key))

    c_gt, c_ge = count(lambda sc, _: sc > thr, lambda sc, _: sc >= thr)
    need = top_k - c_gt
    j_scr[...] = jnp.full((1, tq), seq, I32)
    has_excess = jnp.max(jnp.where((c_ge > top_k) & (n_finite >= top_k), 1, 0)) > 0

    @pl.when(has_excess)
    def _():
        nbits = seq.bit_length()

        def jbit(b, jp):
            cand = jp | (1 << (nbits - 1 - b))
            cnt = count(lambda sc, pos: (sc == thr) & (pos < cand))
            return jnp.where(cnt <= need, cand, jp)

        j_scr[...] = lax.fori_loop(0, nbits, jbit, jnp.zeros((1, tq), I32))

    jlim = j_scr[...]

    def mask_tile(j, _):
        sc = sc_scr[tile_rows(j), :]
        sel = ((sc > thr) | ((sc == thr) & (j * tk + row_iota < jlim))) & (sc > -jnp.inf)
        o_ref[tile_rows(j), :] = jnp.where(sel, 1, 0).astype(o_ref.dtype)
        return 0

    lax.fori_loop(0, n_tiles, mask_tile, 0)

    def zero_tile(j, _):
        o_ref[tile_rows(j), :] = jnp.zeros((tk, tq), o_ref.dtype)
        return 0

    lax.fori_loop(n_tiles, seq // tk, zero_tile, 0)


def _idx_topk(ik, iqt, iwt, *, top_k, tq=256, tk=512, acc_rows=2 * SUBLANES, slab=8 * SUBLANES):
    seq = ik.shape[0]
    tq = min(tq, seq)
    tk = min(tk, seq)
    return pl.pallas_call(
        functools.partial(_idx_topk_body, tq=tq, tk=tk, top_k=top_k, acc_rows=acc_rows, slab=slab),
        grid=(seq // tq,),
        in_specs=[pl.BlockSpec((seq, HEAD_DIM), lambda i: (0, 0)),
                  pl.BlockSpec((IDX_HEADS * HEAD_DIM, tq), lambda i: (0, i)),
                  pl.BlockSpec((IDX_HEADS, tq), lambda i: (0, i))],
        out_specs=pl.BlockSpec((seq, tq), lambda i: (0, i)),
        out_shape=jax.ShapeDtypeStruct((seq, seq), MASK_DTYPE),
        scratch_shapes=[pltpu.VMEM((seq, tq), F32),
                        pltpu.VMEM((seq, tq), BF16),
                        pltpu.VMEM((seq, tq), I16),
                        pltpu.VMEM((1, tq), I32)],
        compiler_params=_cparams("parallel"),
        name="idx_topk",
    )(ik, iqt, iwt)


VT_ROWS = HEAD_DIM + 16


def _attn_body(qi_ref, kj_ref, q_ref, k_ref, vt_ref, m_ref, o_ref, m_scr, l_scr, acc_scr, *, tq, tk, ahead, behind):
    step = pl.program_id(0)
    i = qi_ref[step]
    j = kj_ref[step]

    @pl.when(j == 0)
    def _():
        m_scr[...] = jnp.full_like(m_scr, NEG_BIG)
        l_scr[...] = jnp.zeros_like(l_scr)
        acc_scr[...] = jnp.zeros_like(acc_scr)

    cap = jnp.where(m_ref[...].astype(I32) != 0, jnp.inf, -jnp.inf)
    q = q_ref[...]
    k = k_ref[...]
    hsl = [slice(h * HEAD_DIM, (h + 1) * HEAD_DIM) for h in range(ATT_HEADS)]

    def scores(h):
        return _dot_nt(k[:, hsl[h]], q[:, hsl[h]])

    def accumulate(h, p, alpha):
        pv = _dot(vt_ref[h * VT_ROWS:(h + 1) * VT_ROWS, :], p)
        l_scr[h:h + 1, :] = alpha * l_scr[h:h + 1, :] + pv[HEAD_DIM:HEAD_DIM + 1]
        acc_scr[h] = alpha * acc_scr[h] + pv[0:HEAD_DIM]

    s_raw = [scores(h) for h in range(ahead)]
    pending = []
    for h in range(ATT_HEADS):
        if h + ahead < ATT_HEADS:
            s_raw.append(scores(h + ahead))
        s = jnp.minimum(s_raw[h], cap)
        m_old = m_scr[h:h + 1, :]
        m_new = jnp.maximum(m_old, jnp.max(s, axis=0, keepdims=True))
        m_scr[h:h + 1, :] = m_new
        pending.append((h, jnp.exp2(s - m_new).astype(BF16), jnp.exp2(m_old - m_new)))
        if len(pending) > behind:
            accumulate(*pending.pop(0))
    for item in pending:
        accumulate(*item)

    @pl.when(j == ((i + 1) * tq - 1) // tk)
    def _():
        for h in range(ATT_HEADS):
            out_t = acc_scr[h] / l_scr[h:h + 1, :]
            o_ref[:, h * HEAD_DIM:(h + 1) * HEAD_DIM] = out_t.T.astype(o_ref.dtype)


def _attention(qkv, vt, mask_t, *, tq=256, tk=512, ahead=4, behind=0):
    seq = qkv.shape[0]
    d = ATT_HEADS * HEAD_DIM
    tq = min(tq, seq)
    tk = min(tk, seq)
    pairs = [(i, j) for i in range(seq // tq) for j in range(((i + 1) * tq - 1) // tk + 1)]
    qi = jnp.asarray([p[0] for p in pairs], I32)
    kj = jnp.asarray([p[1] for p in pairs], I32)
    grid_spec = pltpu.PrefetchScalarGridSpec(
        num_scalar_prefetch=2,
        grid=(len(pairs),),
        in_specs=[pl.BlockSpec((tq, d), lambda s, qi, kj: (qi[s], 0)),
                  pl.BlockSpec((tk, d), lambda s, qi, kj: (kj[s], 1)),
                  pl.BlockSpec((ATT_HEADS * VT_ROWS, tk), lambda s, qi, kj: (0, kj[s])),
                  pl.BlockSpec((tk, tq), lambda s, qi, kj: (kj[s], qi[s]))],
        out_specs=pl.BlockSpec((tq, d), lambda s, qi, kj: (qi[s], 0)),
        scratch_shapes=[pltpu.VMEM((ATT_HEADS, tq), F32),
                        pltpu.VMEM((ATT_HEADS, tq), F32),
                        pltpu.VMEM((ATT_HEADS, HEAD_DIM, tq), F32)])
    return pl.pallas_call(
        functools.partial(_attn_body, tq=tq, tk=tk, ahead=ahead, behind=behind),
        grid_spec=grid_spec,
        out_shape=jax.ShapeDtypeStruct((seq, d), BF16),
        compiler_params=_cparams("arbitrary"),
        name="dsa_attention",
    )(qi, kj, qkv, qkv, vt, mask_t)


def _merge_body(x_ref, ya_ref, yb_ref, g_ref, woa_ref, wob_ref, wo_ref, o_ref):
    d = x_ref.shape[1]
    g = g_ref[...]
    merged = (jax.nn.sigmoid(g[:, 0:d]) * _dot(ya_ref[...], woa_ref[...])
              + jax.nn.sigmoid(g[:, d:2 * d]) * _dot(yb_ref[...], wob_ref[...]))
    o_ref[...] = x_ref[...] + _dot(merged.astype(BF16), wo_ref[...])


def _merge(x, ya, yb, gates, woa, wob, wo, *, tm=512):
    m, d = x.shape
    tm = min(tm, m)
    row = lambda n: pl.BlockSpec((tm, n), lambda i: (i, 0))
    wspec = pl.BlockSpec((d, d), lambda i: (0, 0))
    return pl.pallas_call(
        _merge_body,
        grid=(m // tm,),
        in_specs=[row(d), row(d), row(d), row(2 * d), wspec, wspec, wspec],
        out_specs=row(d),
        out_shape=jax.ShapeDtypeStruct((m, d), F32),
        compiler_params=_cparams("parallel"),
        name="merge",
    )(x, ya, yb, gates, woa, wob, wo)


def _layer(x, ffn1_norm, ffn1_w_gate, ffn1_w_up, ffn1_w_down, mix_norm, w_in, time_mix,
           decay_lora, decay_base, aaa_lora, aaa_base, gate_lora, k_k, k_a, r_k, gn_weight, gn_bias,
           w_out_rwkv, w_out_attn, w_out, ffn2_norm, ffn2_w_gate, ffn2_w_up, ffn2_w_down, out_norm,
           *, final):
    seq, d = x.shape
    bf = lambda w: w.astype(BF16)
    x1, hm = _ffn(x, ffn1_norm, bf(ffn1_w_gate), bf(ffn1_w_up), bf(ffn1_w_down), mix_norm, final=False)

    d_rwkv = RWKV_HEADS * HEAD_DIM
    d_att = ATT_HEADS * HEAD_DIM
    n_shift = 3 * d_rwkv + 2 * LANES
    o_qkv = n_shift
    o_iq = o_qkv + 3 * d_att
    o_ik = o_iq + IDX_HEADS * HEAD_DIM
    o_gate = o_ik + HEAD_DIM + IDX_HEADS
    w_ikw = jnp.pad(w_in[:, o_ik:o_gate], ((0, 0), (0, LANES - HEAD_DIM - IDX_HEADS)))
    p_rwkv = _matmul(hm, bf(w_in[:, :n_shift]), F32, tn=n_shift // 2, name="proj_rwkv")
    q_scale = (HEAD_DIM ** -0.5) * 1.4426950408889634
    w_qkv = jnp.concatenate([w_in[:, o_qkv:o_qkv + d_att] * q_scale, w_in[:, o_qkv + d_att:o_iq]], axis=1)
    qkv = _matmul(hm, bf(w_qkv), BF16, tn=1024, name="proj_qkv")
    iq = _matmul(hm, bf(w_in[:, o_iq:o_ik]), BF16, name="proj_iq")
    ikw = _matmul(hm, bf(w_ikw), F32, name="proj_ikw")
    gates = _matmul(hm, bf(w_in[:, o_gate:]), F32, tn=1024, name="proj_gates")

    zpad = jnp.zeros((HEAD_DIM, d_rwkv), F32)
    wlora = bf(jnp.concatenate([decay_lora, zpad], axis=0))
    alora = bf(jnp.concatenate([zpad, aaa_lora], axis=0))
    ya = _rwkv(p_rwkv, time_mix, wlora, decay_base, alora, aaa_base, bf(gate_lora),
               k_k, k_a, r_k.reshape(-1), gn_weight, gn_bias)

    top_k = min(INDEX_TOPK, seq // 4)
    ik = bf(ikw[:, :HEAD_DIM])
    iwt = ikw[:, HEAD_DIM:HEAD_DIM + IDX_HEADS].T
    mask_t = _idx_topk(ik, iq.T, iwt, top_k=top_k)
    vt = qkv[:, 2 * d_att:].T.reshape(ATT_HEADS, HEAD_DIM, seq)
    ones = jnp.ones((ATT_HEADS, VT_ROWS - HEAD_DIM, seq), BF16)
    vt_aug = jnp.concatenate([vt, ones], axis=1).reshape(ATT_HEADS * VT_ROWS, seq)
    yb = _attention(qkv, vt_aug, mask_t)

    x2 = _merge(x1, ya, yb, gates, bf(w_out_rwkv), bf(w_out_attn), bf(w_out))
    return _ffn(x2, ffn2_norm, bf(ffn2_w_gate), bf(ffn2_w_up), bf(ffn2_w_down), out_norm, final=final)


def kernel(x, ffn1_norm, ffn1_w_gate, ffn1_w_up, ffn1_w_down, mix_norm, w_in, time_mix, decay_lora, decay_base, aaa_lora, aaa_base, gate_lora, k_k, k_a, r_k, gn_weight, gn_bias, w_out_rwkv, w_out_attn, w_out, ffn2_norm, ffn2_w_gate, ffn2_w_up, ffn2_w_down, final_norm):
    bsz, seq, d = x.shape
    depth = w_in.shape[0]
    assert bsz == 1 and depth == 1
    y = _layer(x[0], ffn1_norm[0], ffn1_w_gate[0], ffn1_w_up[0], ffn1_w_down[0], mix_norm[0], w_in[0],
               time_mix[0], decay_lora[0], decay_base[0], aaa_lora[0], aaa_base[0], gate_lora[0],
               k_k[0], k_a[0], r_k[0], gn_weight[0], gn_bias[0], w_out_rwkv[0], w_out_attn[0], w_out[0],
               ffn2_norm[0], ffn2_w_gate[0], ffn2_w_up[0], ffn2_w_down[0], final_norm, final=True)
    return y[None]
```
